```python
import jax, jax.numpy as jnp
from jax import lax
import numpy as np

D_MODEL = 1024
BATCH = 4
SEQ = 8192
DEPTH = 1
DEC_BATCH = 32
DEC_SEQ = 64
PAST_LEN = 1024

CHUNK = 64
D_POOL = D_MODEL // 2
POOL_WINDOWS = (2, 4, 8, 16)
N_POOL_GROUPS = 4
POOL_GROUP = D_POOL // N_POOL_GROUPS
POOL_STATE = 15
D_SGU = D_MODEL // 2
N_SGU_GROUPS = 4
SGU_GROUP = D_SGU // N_SGU_GROUPS
SGU_CHUNK = 128
D_IN = 2 * D_POOL + 3 * D_SGU + 2 * D_MODEL
EPS = 1e-6

kernel_name = "pool_sgu_gated_stream_encoder"


def _rmsnorm(x, g):
    xf = x.astype(jnp.float32)
    y = xf * lax.rsqrt(jnp.mean(xf * xf, axis=-1, keepdims=True) + EPS)
    return (y * g.astype(jnp.float32)).astype(x.dtype)


def _layernorm(x, g, b):
    xf = x.astype(jnp.float32)
    mu = jnp.mean(xf, axis=-1, keepdims=True)
    var = jnp.mean(jnp.square(xf - mu), axis=-1, keepdims=True)
    y = (xf - mu) * lax.rsqrt(var + EPS)
    return (y * g.astype(jnp.float32) + b.astype(jnp.float32)).astype(x.dtype)


def _pool_mix(a, prev, pos0):
    L = a.shape[1]
    ap = jnp.concatenate([prev.astype(a.dtype), a], axis=1).astype(jnp.float32)
    c = jnp.cumsum(ap, axis=1)
    c = jnp.concatenate([jnp.zeros_like(c[:, :1]), c], axis=1)
    pos = pos0 + jnp.arange(L)
    outs = []
    for g, w in enumerate(POOL_WINDOWS):
        lo, hi = g * POOL_GROUP, (g + 1) * POOL_GROUP
        s = POOL_STATE + 1
        win = c[:, s:s + L, lo:hi] - c[:, s - w:s - w + L, lo:hi]
        cnt = jnp.minimum(pos + 1, w).astype(jnp.float32)[None, :, None]
        outs.append(win / cnt)
    mean = jnp.concatenate(outs, axis=-1)
    return (mean - a.astype(jnp.float32)).astype(a.dtype)


def _spatial_gate(u, v, w_s, b_s, chunk_len):
    B, L, _ = v.shape
    n = L // chunk_len
    blk = jnp.arange(chunk_len) // CHUNK
    mask = blk[:, None] >= blk[None, :]
    w = jnp.where(mask[None], w_s[:, :chunk_len, :chunk_len], 0.0)
    vr = v.reshape(B, n, chunk_len, N_SGU_GROUPS, SGU_GROUP)
    z = jnp.einsum('gij,bnjgc->bnigc', w, vr)
    z = z + b_s[:, :chunk_len].T[None, None, :, :, None]
    return u * z.reshape(B, L, D_SGU)


def _layer(x, prev_pool, pos0, sgu_chunk, norm_g, w_in, w_pool, pool_scale, ln_g, ln_b, w_s, b_s, w_a, w_b, w_o):
    B, L, _ = x.shape
    h = _rmsnorm(x, norm_g)
    proj = h @ w_in
    sizes = (D_POOL, D_POOL, D_SGU, D_SGU, D_SGU, D_MODEL, D_MODEL)
    idx = np.cumsum(sizes)[:-1].tolist()
    a, g_a, u, v, g_b, r_a, r_b = jnp.split(proj, idx, axis=-1)
    pa = _pool_mix(a, prev_pool, pos0).reshape(B, L, N_POOL_GROUPS, POOL_GROUP)
    pa = jnp.einsum('blgc,gcd->blgd', pa, w_pool).reshape(B, L, D_POOL) * pool_scale
    y_a = (jax.nn.silu(g_a) * pa) @ w_a
    v_hat = _layernorm(v, ln_g, ln_b)
    y_b = (jax.nn.silu(g_b) * _spatial_gate(u, v_hat, w_s, b_s, sgu_chunk)) @ w_b
    merged = jax.nn.sigmoid(r_a) * y_a + jax.nn.sigmoid(r_b) * y_b
    return x + merged @ w_o, a, v_hat


def setup_inputs(seed: int = 0) -> dict:
    key = jax.random.key(seed)
    ks = jax.random.split(key, 16)
    f32 = jnp.float32

    def nrm(k, shape, scale):
        return scale * jax.random.normal(k, shape, f32)

    return {
        "x_prompt": nrm(ks[0], (BATCH, SEQ, D_MODEL), 1.0),
        "x_sample": nrm(ks[1], (DEC_BATCH, DEC_SEQ, D_MODEL), 1.0),
        "cache_pool": nrm(ks[2], (DEPTH, DEC_BATCH, POOL_STATE, D_POOL), 1.0),
        "norm_g": 1.0 + nrm(ks[3], (DEPTH, D_MODEL), 0.05),
        "w_in": nrm(ks[4], (DEPTH, D_MODEL, D_IN), D_MODEL ** -0.5),
        "w_pool": nrm(ks[5], (DEPTH, N_POOL_GROUPS, POOL_GROUP, POOL_GROUP), POOL_GROUP ** -0.5),
        "pool_scale": 1.0 + nrm(ks[6], (DEPTH, D_POOL), 0.1),
        "ln_g": 1.0 + nrm(ks[7], (DEPTH, D_SGU), 0.05),
        "ln_b": nrm(ks[8], (DEPTH, D_SGU), 0.02),
        "w_s": nrm(ks[9], (DEPTH, N_SGU_GROUPS, SGU_CHUNK, SGU_CHUNK), SGU_CHUNK ** -0.5),
        "b_s": 1.0 + nrm(ks[10], (DEPTH, N_SGU_GROUPS, SGU_CHUNK), 0.1),
        "w_a": nrm(ks[11], (DEPTH, D_POOL, D_MODEL), D_POOL ** -0.5),
        "w_b": nrm(ks[12], (DEPTH, D_SGU, D_MODEL), D_SGU ** -0.5),
        "w_o": nrm(ks[13], (DEPTH, D_MODEL, D_MODEL), D_MODEL ** -0.5),
        "final_g": 1.0 + nrm(ks[14], (D_MODEL,), 0.05),
    }


def reference(x_prompt, x_sample, cache_pool, norm_g, w_in, w_pool, pool_scale, ln_g, ln_b, w_s, b_s,
              w_a, w_b, w_o, final_g):
    xp, xs = x_prompt, x_sample
    pool_p, pool_s, v_s = [], [], []
    for l in range(DEPTH):
        params = (norm_g[l], w_in[l], w_pool[l], pool_scale[l], ln_g[l], ln_b[l], w_s[l], b_s[l],
                  w_a[l], w_b[l], w_o[l])
        no_hist = jnp.zeros((xp.shape[0], POOL_STATE, D_POOL), xp.dtype)
        xp, a_p, _ = _layer(xp, no_hist, 0, SGU_CHUNK, *params)
        xs, a_s, vh_s = _layer(xs, cache_pool[l], PAST_LEN, xs.shape[1], *params)
        pool_p.append(a_p[:, -POOL_STATE:])
        pool_s.append(jnp.concatenate([cache_pool[l].astype(a_s.dtype), a_s], axis=1)[:, -POOL_STATE:])
        v_s.append(vh_s)
    y_prompt = _rmsnorm(xp, final_g)
    y_sample = _rmsnorm(xs, final_g)
    state_pool_prompt = jnp.stack(pool_p)
    state_pool_sample = jnp.stack(pool_s)
    state_v_sample = jnp.stack(v_s)
    return (y_prompt, y_sample, state_pool_prompt, state_pool_sample, state_v_sample)
```

```python
import functools

import jax
import jax.numpy as jnp
from jax import lax
from jax.experimental import pallas as pl
from jax.experimental.pallas import tpu as pltpu

EPS = 1e-6
PAST_LEN = 1024
CHUNK = 64
SGU_CHUNK = 128
POOL_WINDOWS = (2, 4, 8, 16)
POOL_STATE = 15
HIST_ROWS = 16
LANES = 128

PROMPT_TILE = 512
SAMPLE_STREAMS_PER_TILE = 8
VMEM_LIMIT_BYTES = 56 * 1024 * 1024


def _rmsnorm(x, g):
    ms = jnp.mean(x * x, axis=-1, keepdims=True)
    return x * lax.rsqrt(ms + EPS) * g


def _layernorm(x, g, b):
    mu = jnp.mean(x, axis=-1, keepdims=True)
    xc = x - mu
    var = jnp.mean(xc * xc, axis=-1, keepdims=True)
    return xc * lax.rsqrt(var + EPS) * g + b


def _dot(a, b):
    return jnp.dot(a, b, preferred_element_type=jnp.float32)


def _pool_mix(a, hist, pos0):
    seg_len = a.shape[0]
    ext = jnp.concatenate([hist, a], axis=0)
    pos = pos0 + lax.broadcasted_iota(jnp.int32, (seg_len, LANES), 0)
    outs = []
    for g, w in enumerate(POOL_WINDOWS):
        win = ext[:, g * LANES:(g + 1) * LANES]
        span = 1
        while span < w:
            win = win + pltpu.roll(win, span, 0)
            span *= 2
        cnt = jnp.minimum(pos + 1, w).astype(jnp.float32)
        outs.append(win[HIST_ROWS:] / cnt - a[:, g * LANES:(g + 1) * LANES])
    return jnp.concatenate(outs, axis=-1)


def _encoder_tile(cfg, x, hists, pos0, refs):
    (norm_g, w_in, w_pool, pool_scale, ln_g, ln_b, w_s, bias_s, w_a, w_b, w_o, final_g) = refs
    n_seg, seg_len, chunk_len = cfg
    d_model = x.shape[-1]
    d_pool = w_a.shape[0]
    d_sgu = w_b.shape[0]
    n_groups = w_s.shape[0]
    bf16 = jnp.bfloat16

    h = _rmsnorm(x, norm_g[...]).astype(bf16)

    def proj(c0, width):
        return _dot(h, w_in[:, c0:c0 + width])

    a = proj(0, d_pool)
    g_a = proj(d_pool, d_pool)
    pa = jnp.concatenate(
        [_pool_mix(a[s * seg_len:(s + 1) * seg_len], hists[s], pos0) for s in range(n_seg)], axis=0)
    pa = pa.astype(bf16)
    pa = jnp.concatenate(
        [_dot(pa[:, g * LANES:(g + 1) * LANES], w_pool[g]) for g in range(len(POOL_WINDOWS))], axis=-1)
    pa = pa * pool_scale[...]
    y_a = _dot((jax.nn.silu(g_a) * pa).astype(bf16), w_a[...])

    c0 = 2 * d_pool
    u = proj(c0, d_sgu)
    v = proj(c0 + d_sgu, d_sgu)
    g_b = proj(c0 + 2 * d_sgu, d_sgu)
    v_hat = _layernorm(v, ln_g[...], ln_b[...])
    vb = v_hat.astype(bf16)
    blk = lax.broadcasted_iota(jnp.int32, (chunk_len, chunk_len), 0) // CHUNK
    blk_t = lax.broadcasted_iota(jnp.int32, (chunk_len, chunk_len), 1) // CHUNK
    mask = blk >= blk_t
    w_mix = [jnp.where(mask, w_s[g, :chunk_len, :chunk_len], 0.0).astype(bf16) for g in range(n_groups)]
    bias = bias_s[:chunk_len, :]
    z_rows = []
    for c in range(n_seg * seg_len // chunk_len):
        vc = vb[c * chunk_len:(c + 1) * chunk_len]
        zc = jnp.concatenate(
            [_dot(w_mix[g], vc[:, g * LANES:(g + 1) * LANES]) for g in range(n_groups)], axis=-1)
        z_rows.append(zc + bias)
    z = jnp.concatenate(z_rows, axis=0)
    y_b = _dot((jax.nn.silu(g_b) * (u * z)).astype(bf16), w_b[...])

    c0 = 2 * d_pool + 3 * d_sgu
    r_a = proj(c0, d_model)
    r_b = proj(c0 + d_model, d_model)
    merged = jax.nn.sigmoid(r_a) * y_a + jax.nn.sigmoid(r_b) * y_b
    out = x + _dot(merged.astype(bf16), w_o[...])
    return _rmsnorm(out, final_g[...]), a, v_hat


def _prompt_kernel(tile, x_ref, *rest):
    refs, (y_ref, pool_ref), (hist_ref,) = rest[:12], rest[12:14], rest[14:]
    t = pl.program_id(1)

    @pl.when(t == 0)
    def _():
        hist_ref[...] = jnp.zeros_like(hist_ref)

    cfg = (1, tile, SGU_CHUNK)
    y, a, _ = _encoder_tile(cfg, x_ref[0], [hist_ref[...]], t * tile, refs)
    y_ref[0] = y
    hist_ref[...] = a[tile - HIST_ROWS:]
    pool_ref[0, 0] = a[tile - POOL_STATE:]


def _sample_kernel(cfg, pos0, x_ref, cache_ref, *rest):
    refs, (y_ref, pool_ref, v_ref) = rest[:12], rest[12:]
    n_seg, seg_len, _ = cfg
    d_model = x_ref.shape[-1]
    x = x_ref[...].reshape(n_seg * seg_len, d_model)
    y, a, v_hat = _encoder_tile(cfg, x, [cache_ref[s] for s in range(n_seg)], pos0, refs)
    y_ref[...] = y.reshape(n_seg, seg_len, d_model)
    a = a.reshape(n_seg, seg_len, a.shape[-1])
    pool_ref[0] = a[:, seg_len - POOL_STATE:]
    v_ref[0] = v_hat.reshape(n_seg, seg_len, v_hat.shape[-1])


def _const_spec(arr):
    zeros = (0,) * arr.ndim
    return pl.BlockSpec(arr.shape, lambda *_: zeros, pipeline_mode=pl.Buffered(1))


def _layer_params(l, norm_g, w_in, w_pool, pool_scale, ln_g, ln_b, w_s, b_s, w_a, w_b, w_o, final_g):
    bf16 = jnp.bfloat16
    bias_s = jnp.repeat(b_s[l].T, LANES, axis=1)
    return (norm_g[l][None], w_in[l].astype(bf16), w_pool[l].astype(bf16), pool_scale[l][None],
            ln_g[l][None], ln_b[l][None], w_s[l], bias_s, w_a[l].astype(bf16), w_b[l].astype(bf16),
            w_o[l].astype(bf16), final_g[None])


def kernel(x_prompt, x_sample, cache_pool, norm_g, w_in, w_pool, pool_scale, ln_g, ln_b, w_s, b_s, w_a, w_b, w_o,
           final_g):
    depth = w_in.shape[0]
    assert depth == 1, "the fused kernel ends with the final norm, so it covers a single-layer trunk"
    batch, seq, d_model = x_prompt.shape
    dec_batch, dec_seq, _ = x_sample.shape
    d_pool = w_a.shape[1]
    d_sgu = w_b.shape[1]
    assert d_pool == len(POOL_WINDOWS) * LANES and d_sgu == w_s.shape[1] * LANES
    assert dec_seq >= POOL_STATE and dec_seq % CHUNK == 0
    f32 = jnp.float32

    params = _layer_params(0, norm_g, w_in, w_pool, pool_scale, ln_g, ln_b, w_s, b_s, w_a, w_b, w_o, final_g)
    param_specs = [_const_spec(p) for p in params]
    cparams = dict(vmem_limit_bytes=VMEM_LIMIT_BYTES)

    tile = PROMPT_TILE
    assert seq % tile == 0 and tile % SGU_CHUNK == 0
    y_prompt, pool_p = pl.pallas_call(
        functools.partial(_prompt_kernel, tile),
        out_shape=(jax.ShapeDtypeStruct((batch, seq, d_model), f32),
                   jax.ShapeDtypeStruct((depth, batch, POOL_STATE, d_pool), f32)),
        grid=(batch, seq // tile),
        in_specs=[pl.BlockSpec((1, tile, d_model), lambda b, t: (b, t, 0))] + param_specs,
        out_specs=(pl.BlockSpec((1, tile, d_model), lambda b, t: (b, t, 0)),
                   pl.BlockSpec((1, 1, POOL_STATE, d_pool), lambda b, t: (0, b, 0, 0))),
        scratch_shapes=[pltpu.VMEM((HIST_ROWS, d_pool), f32)],
        compiler_params=pltpu.CompilerParams(dimension_semantics=("arbitrary", "arbitrary"), **cparams),
        name="encoder_prompt",
    )(x_prompt, *params)

    n_seg = SAMPLE_STREAMS_PER_TILE
    assert dec_batch % n_seg == 0
    cache = jnp.pad(cache_pool[0], ((0, 0), (HIST_ROWS - POOL_STATE, 0), (0, 0)))
    y_sample, pool_s, v_s = pl.pallas_call(
        functools.partial(_sample_kernel, (n_seg, dec_seq, dec_seq), PAST_LEN),
        out_shape=(jax.ShapeDtypeStruct((dec_batch, dec_seq, d_model), f32),
                   jax.ShapeDtypeStruct((depth, dec_batch, POOL_STATE, d_pool), f32),
                   jax.ShapeDtypeStruct((depth, dec_batch, dec_seq, d_sgu), f32)),
        grid=(dec_batch // n_seg,),
        in_specs=[pl.BlockSpec((n_seg, dec_seq, d_model), lambda i: (i, 0, 0)),
                  pl.BlockSpec((n_seg, HIST_ROWS, d_pool), lambda i: (i, 0, 0))] + param_specs,
        out_specs=(pl.BlockSpec((n_seg, dec_seq, d_model), lambda i: (i, 0, 0)),
                   pl.BlockSpec((1, n_seg, POOL_STATE, d_pool), lambda i: (0, i, 0, 0)),
                   pl.BlockSpec((1, n_seg, dec_seq, d_sgu), lambda i: (0, i, 0, 0))),
        compiler_params=pltpu.CompilerParams(dimension_semantics=("arbitrary",), **cparams),
        name="encoder_sample",
    )(x_sample, cache, *params)
    return (y_prompt, y_sample, pool_p, pool_s, v_s)
```

```python
import functools

import jax
import jax.numpy as jnp
from jax import lax
from jax.experimental import pallas as pl
from jax.experimental.pallas import tpu as pltpu

EPS = 1e-6
PAST_LEN = 1024
CHUNK = 64
SGU_CHUNK = 128
POOL_WINDOWS = (2, 4, 8, 16)
POOL_STATE = 15
HIST_ROWS = 16
LANES = 128

PROMPT_TILE = 1024
SAMPLE_STREAMS_PER_TILE = 8
VMEM_LIMIT_BYTES = 56 * 1024 * 1024


def _rmsnorm(x, g):
    ms = jnp.mean(x * x, axis=-1, keepdims=True)
    return x * lax.rsqrt(ms + EPS) * g


def _layernorm(x, g, b):
    mu = jnp.mean(x, axis=-1, keepdims=True)
    xc = x - mu
    var = jnp.mean(xc * xc, axis=-1, keepdims=True)
    return xc * lax.rsqrt(var + EPS) * g + b


def _dot(a, b):
    return jnp.dot(a, b, preferred_element_type=jnp.float32)


def _pool_mix(a, hist, pos0):
    seg_len = a.shape[0]
    ext = jnp.concatenate([hist, a], axis=0)
    pos = pos0 + lax.broadcasted_iota(jnp.int32, (seg_len, LANES), 0)
    outs = []
    for g, w in enumerate(POOL_WINDOWS):
        win = ext[:, g * LANES:(g + 1) * LANES]
        span = 1
        while span < w:
            win = win + pltpu.roll(win, span, 0)
            span *= 2
        cnt = jnp.minimum(pos + 1, w).astype(jnp.float32)
        outs.append(win[HIST_ROWS:] / cnt - a[:, g * LANES:(g + 1) * LANES])
    return jnp.concatenate(outs, axis=-1)


def _encoder_tile(cfg, x, hists, pos0, refs):
    (norm_g, w_in, w_pool, pool_scale, ln_g, ln_b, w_s, bias_s, w_a, w_b, w_o, final_g) = refs
    n_seg, seg_len, chunk_len = cfg
    d_model = x.shape[-1]
    d_pool = w_a.shape[0]
    d_sgu = w_b.shape[0]
    n_groups = w_s.shape[0]
    bf16 = jnp.bfloat16

    h = _rmsnorm(x, norm_g[...]).astype(bf16)

    def proj(c0, width):
        return _dot(h, w_in[:, c0:c0 + width])

    c_sgu = 2 * d_pool
    c_gate = 2 * d_pool + 3 * d_sgu
    a = proj(0, d_pool)
    v = proj(c_sgu + d_sgu, d_sgu)
    g_a = proj(d_pool, d_pool)
    u = proj(c_sgu, d_sgu)

    pa = jnp.concatenate(
        [_pool_mix(a[s * seg_len:(s + 1) * seg_len], hists[s], pos0) for s in range(n_seg)], axis=0)
    pa = pa.astype(bf16)
    pa = jnp.concatenate(
        [_dot(pa[:, g * LANES:(g + 1) * LANES], w_pool[g]) for g in range(len(POOL_WINDOWS))], axis=-1)
    ya_in = (jax.nn.silu(g_a) * (pa * pool_scale[...])).astype(bf16)

    g_b = proj(c_sgu + 2 * d_sgu, d_sgu)

    v_hat = _layernorm(v, ln_g[...], ln_b[...])
    vb = v_hat.astype(bf16)
    blk = lax.broadcasted_iota(jnp.int32, (chunk_len, chunk_len), 0) // CHUNK
    blk_t = lax.broadcasted_iota(jnp.int32, (chunk_len, chunk_len), 1) // CHUNK
    mask = blk >= blk_t
    w_mix = [jnp.where(mask, w_s[g, :chunk_len, :chunk_len], 0.0).astype(bf16) for g in range(n_groups)]
    bias = bias_s[:chunk_len, :]
    z_rows = []
    for c in range(n_seg * seg_len // chunk_len):
        vc = vb[c * chunk_len:(c + 1) * chunk_len]
        zc = jnp.concatenate(
            [_dot(w_mix[g], vc[:, g * LANES:(g + 1) * LANES]) for g in range(n_groups)], axis=-1)
        z_rows.append(zc + bias)
    z = jnp.concatenate(z_rows, axis=0)
    yb_in = (jax.nn.silu(g_b) * (u * z)).astype(bf16)

    r_a = proj(c_gate, d_model)
    r_b = proj(c_gate + d_model, d_model)
    y_a = _dot(ya_in, w_a[...])
    y_b = _dot(yb_in, w_b[...])

    merged = jax.nn.sigmoid(r_a) * y_a + jax.nn.sigmoid(r_b) * y_b
    out = x + _dot(merged.astype(bf16), w_o[...])
    return _rmsnorm(out, final_g[...]), a, v_hat


def _prompt_kernel(tile, x_ref, *rest):
    refs, (y_ref, pool_ref), (hist_ref,) = rest[:12], rest[12:14], rest[14:]
    t = pl.program_id(1)

    @pl.when(t == 0)
    def _():
        hist_ref[...] = jnp.zeros_like(hist_ref)

    cfg = (1, tile, SGU_CHUNK)
    y, a, _ = _encoder_tile(cfg, x_ref[0], [hist_ref[...]], t * tile, refs)
    y_ref[0] = y
    hist_ref[...] = a[tile - HIST_ROWS:]
    pool_ref[0, 0] = a[tile - POOL_STATE:]


def _sample_kernel(cfg, pos0, x_ref, cache_ref, *rest):
    refs, (y_ref, pool_ref, v_ref) = rest[:12], rest[12:]
    n_seg, seg_len, _ = cfg
    d_model = x_ref.shape[-1]
    x = x_ref[...].reshape(n_seg * seg_len, d_model)
    y, a, v_hat = _encoder_tile(cfg, x, [cache_ref[s] for s in range(n_seg)], pos0, refs)
    y_ref[...] = y.reshape(n_seg, seg_len, d_model)
    a = a.reshape(n_seg, seg_len, a.shape[-1])
    pool_ref[0] = a[:, seg_len - POOL_STATE:]
    v_ref[0] = v_hat.reshape(n_seg, seg_len, v_hat.shape[-1])


def _const_spec(arr):
    zeros = (0,) * arr.ndim
    return pl.BlockSpec(arr.shape, lambda *_: zeros, pipeline_mode=pl.Buffered(1))


def _layer_params(l, norm_g, w_in, w_pool, pool_scale, ln_g, ln_b, w_s, b_s, w_a, w_b, w_o, final_g):
    bf16 = jnp.bfloat16
    bias_s = jnp.repeat(b_s[l].T, LANES, axis=1)
    return (norm_g[l][None], w_in[l].astype(bf16), w_pool[l].astype(bf16), pool_scale[l][None],
            ln_g[l][None], ln_b[l][None], w_s[l], bias_s, w_a[l].astype(bf16), w_b[l].astype(bf16),
            w_o[l].astype(bf16), final_g[None])


def kernel(x_prompt, x_sample, cache_pool, norm_g, w_in, w_pool, pool_scale, ln_g, ln_b, w_s, b_s, w_a, w_b, w_o,
           final_g):
    depth = w_in.shape[0]
    assert depth == 1, "the fused kernel ends with the final norm, so it covers a single-layer trunk"
    batch, seq, d_model = x_prompt.shape
    dec_batch, dec_seq, _ = x_sample.shape
    d_pool = w_a.shape[1]
    d_sgu = w_b.shape[1]
    assert d_pool == len(POOL_WINDOWS) * LANES and d_sgu == w_s.shape[1] * LANES
    assert dec_seq >= POOL_STATE and dec_seq % CHUNK == 0
    f32 = jnp.float32

    params = _layer_params(0, norm_g, w_in, w_pool, pool_scale, ln_g, ln_b, w_s, b_s, w_a, w_b, w_o, final_g)
    param_specs = [_const_spec(p) for p in params]
    cparams = dict(vmem_limit_bytes=VMEM_LIMIT_BYTES)

    tile = PROMPT_TILE
    assert seq % tile == 0 and tile % SGU_CHUNK == 0
    y_prompt, pool_p = pl.pallas_call(
        functools.partial(_prompt_kernel, tile),
        out_shape=(jax.ShapeDtypeStruct((batch, seq, d_model), f32),
                   jax.ShapeDtypeStruct((depth, batch, POOL_STATE, d_pool), f32)),
        grid=(batch, seq // tile),
        in_specs=[pl.BlockSpec((1, tile, d_model), lambda b, t: (b, t, 0))] + param_specs,
        out_specs=(pl.BlockSpec((1, tile, d_model), lambda b, t: (b, t, 0)),
                   pl.BlockSpec((1, 1, POOL_STATE, d_pool), lambda b, t: (0, b, 0, 0))),
        scratch_shapes=[pltpu.VMEM((HIST_ROWS, d_pool), f32)],
        compiler_params=pltpu.CompilerParams(dimension_semantics=("arbitrary", "arbitrary"), **cparams),
        name="encoder_prompt",
    )(x_prompt, *params)

    n_seg = SAMPLE_STREAMS_PER_TILE
    assert dec_batch % n_seg == 0
    cache = jnp.pad(cache_pool[0], ((0, 0), (HIST_ROWS - POOL_STATE, 0), (0, 0)))
    y_sample, pool_s, v_s = pl.pallas_call(
        functools.partial(_sample_kernel, (n_seg, dec_seq, dec_seq), PAST_LEN),
        out_shape=(jax.ShapeDtypeStruct((dec_batch, dec_seq, d_model), f32),
                   jax.ShapeDtypeStruct((depth, dec_batch, POOL_STATE, d_pool), f32),
                   jax.ShapeDtypeStruct((depth, dec_batch, dec_seq, d_sgu), f32)),
        grid=(dec_batch // n_seg,),
        in_specs=[pl.BlockSpec((n_seg, dec_seq, d_model), lambda i: (i, 0, 0)),
                  pl.BlockSpec((n_seg, HIST_ROWS, d_pool), lambda i: (i, 0, 0))] + param_specs,
        out_specs=(pl.BlockSpec((n_seg, dec_seq, d_model), lambda i: (i, 0, 0)),
                   pl.BlockSpec((1, n_seg, POOL_STATE, d_pool), lambda i: (0, i, 0, 0)),
                   pl.BlockSpec((1, n_seg, dec_seq, d_sgu), lambda i: (0, i, 0, 0))),
        compiler_params=pltpu.CompilerParams(dimension_semantics=("arbitrary",), **cparams),
        name="encoder_sample",
    )(x_sample, cache, *params)
    return (y_prompt, y_sample, pool_p, pool_s, v_s)
```
